```python
import math
import jax, jax.numpy as jnp
from jax import lax
import numpy as np

D_MODEL = 1024
BATCH = 32
SEQ = 256
DEPTH = 4
DEC_BATCH = 8
DEC_SEQ = 2048
PAST_LEN = 256

GRID_W = 64
BLK = 128
WINDOW = 128
ROPE_BASE = 10000.0
EPS = 1e-6
NEG_INF = -1e30
D_MIX = D_MODEL
A_WIDTH = D_MIX // 2
B_WIDTH = D_MIX // 4
C_WIDTH = D_MIX // 4
A_HEADS = 4
A_V_DIM = A_WIDTH // A_HEADS
A_QK_DIM = A_V_DIM // 2
B_HEADS = 4
B_KV_HEADS = 2
B_GROUP = B_HEADS // B_KV_HEADS
B_HEAD_DIM = B_WIDTH // B_HEADS
C_GROUPS = 4
C_GROUP_DIM = C_WIDTH // C_GROUPS
A_Q_COLS = A_HEADS * 2 * A_QK_DIM
A_K_COLS = A_HEADS * 2 * A_QK_DIM
A_V_COLS = A_HEADS * A_V_DIM
B_Q_COLS = B_HEADS * B_HEAD_DIM
B_KV_COLS = B_KV_HEADS * B_HEAD_DIM
GATE_COLS = D_MIX
IN_COLS = A_Q_COLS + A_K_COLS + A_V_COLS + B_Q_COLS + 2 * B_KV_COLS + C_WIDTH + GATE_COLS

kernel_name = "hybrid_diffattn_swa_fourier_prefix_dit_step"


def split_points():
    sizes = [A_Q_COLS, A_K_COLS, A_V_COLS, B_Q_COLS, B_KV_COLS, B_KV_COLS, C_WIDTH]
    return [int(v) for v in np.cumsum(sizes)]


def rms_norm(x, g):
    xf = x.astype(jnp.float32)
    y = xf * lax.rsqrt(jnp.mean(xf * xf, axis=-1, keepdims=True) + EPS)
    return (y * g.astype(jnp.float32)).astype(x.dtype)


def axial_rope_tables(n_tokens, dim, dtype):
    rows = n_tokens // GRID_W
    row = jnp.repeat(jnp.arange(rows), GRID_W).astype(jnp.float32)
    col = jnp.tile(jnp.arange(GRID_W), rows).astype(jnp.float32)
    nf = dim // 4
    inv = ROPE_BASE ** (-jnp.arange(nf, dtype=jnp.float32) / nf)
    ar = row[:, None] * inv
    ac = col[:, None] * inv
    ang = jnp.concatenate([ar, ar, ac, ac], axis=-1)
    return jnp.cos(ang).astype(dtype), jnp.sin(ang).astype(dtype)


def apply_rope(x, cos, sin):
    d = x.shape[-1]
    shape = (x.shape[1],) + (1,) * (x.ndim - 3) + (d,)
    c = cos.reshape(shape)
    s = sin.reshape(shape)
    x1a, x1b, x2a, x2b = jnp.split(x, 4, axis=-1)
    rot = jnp.concatenate([-x1b, x1a, -x2b, x2a], axis=-1)
    return x * c + rot * s


def diff_attention(q, k, v, lam):
    b, sq, h, _, dk = q.shape
    nb = sq // BLK
    qb = jnp.moveaxis(q.reshape(b, nb, BLK, h, 2, dk), 1, 0)
    scale = dk ** -0.5

    def one(qblk):
        s = jnp.einsum('bqhcd,bkhcd->bhcqk', qblk, k).astype(jnp.float32) * scale
        p = jax.nn.softmax(s, axis=-1)
        w = p[:, :, 0] - lam * p[:, :, 1]
        return jnp.einsum('bhqk,bkhd->bqhd', w.astype(v.dtype), v)

    o = lax.map(one, qb)
    return jnp.moveaxis(o, 0, 1).reshape(b, sq, h, v.shape[-1])


def sink_attention(q, k, v, sink):
    b, sq, kvh, g, d = q.shape
    nb = sq // BLK
    qb = jnp.moveaxis(q.reshape(b, nb, BLK, kvh, g, d), 1, 0)
    scale = d ** -0.5

    def one(qblk):
        s = jnp.einsum('bqkgd,bjkd->bkgqj', qblk, k).astype(jnp.float32) * scale
        sk = jnp.broadcast_to(sink.astype(jnp.float32)[None, :, :, None, None], s.shape[:-1] + (1,))
        p = jax.nn.softmax(jnp.concatenate([s, sk], axis=-1), axis=-1)[..., :-1]
        return jnp.einsum('bkgqj,bjkd->bqkgd', p.astype(v.dtype), v)

    o = lax.map(one, qb)
    return jnp.moveaxis(o, 0, 1).reshape(b, sq, kvh * g * d)


def banded_sink_attention(q, k, v, kc, vc, sink):
    b, s_len, kvh, g, d = q.shape
    nb = s_len // BLK
    qb = q.reshape(b, nb, BLK, kvh, g, d)

    def windows(t):
        tb = t.reshape(b, nb, BLK, kvh, d)
        tp = jnp.pad(tb, ((0, 0), (1, 1), (0, 0), (0, 0), (0, 0)))
        return jnp.concatenate([tp[:, :-2], tp[:, 1:-1], tp[:, 2:]], axis=2)

    kw, vw = windows(k), windows(v)
    scale = d ** -0.5
    s_loc = jnp.einsum('bnqkgd,bnjkd->bnkgqj', qb, kw).astype(jnp.float32) * scale
    blk = jnp.arange(nb)[:, None, None]
    qa = jnp.arange(BLK)[None, :, None]
    kj = jnp.arange(3 * BLK)[None, None, :]
    kpos = blk * BLK + kj - BLK
    qpos = blk * BLK + qa
    valid = (jnp.abs(kpos - qpos) <= WINDOW) & (kpos >= 0) & (kpos < s_len)
    s_loc = jnp.where(valid[None, :, None, None], s_loc, NEG_INF)
    s_ctx = jnp.einsum('bnqkgd,bjkd->bnkgqj', qb, kc).astype(jnp.float32) * scale
    sk = jnp.broadcast_to(sink.astype(jnp.float32)[None, None, :, :, None, None], s_loc.shape[:-1] + (1,))
    p = jax.nn.softmax(jnp.concatenate([s_loc, s_ctx, sk], axis=-1), axis=-1)
    p_loc = p[..., :3 * BLK].astype(v.dtype)
    p_ctx = p[..., 3 * BLK:-1].astype(v.dtype)
    o = (jnp.einsum('bnkgqj,bnjkd->bnqkgd', p_loc, vw)
         + jnp.einsum('bnkgqj,bjkd->bnqkgd', p_ctx, vc))
    return o.reshape(b, s_len, kvh * g * d)


def fourier_mix(u, w_f):
    b, s, _ = u.shape
    ug = u.reshape(b, s, C_GROUPS, C_GROUP_DIM).astype(jnp.float32)
    f = jnp.fft.fft2(ug, axes=(1, 3), norm="ortho").real.astype(u.dtype)
    return jnp.einsum('bsgc,gcd->bsgd', f, w_f).reshape(b, s, C_WIDTH)


def mixer_layer(x, shift, scale, gate, norm_g, w_in, lam, lam_init, diff_g, sink, w_f, w_out,
                rope_a=None, rope_b=None, ctx=None):
    b, s, _ = x.shape
    h = rms_norm(x, norm_g) * (1.0 + scale) + shift
    proj = h @ w_in
    qa, ka, va, qbq, kbk, vbv, uc, z = jnp.split(proj, split_points(), axis=-1)
    qa = qa.reshape(b, s, A_HEADS, 2, A_QK_DIM)
    ka = ka.reshape(b, s, A_HEADS, 2, A_QK_DIM)
    va = va.reshape(b, s, A_HEADS, A_V_DIM)
    qbq = qbq.reshape(b, s, B_KV_HEADS, B_GROUP, B_HEAD_DIM)
    kbk = kbk.reshape(b, s, B_KV_HEADS, B_HEAD_DIM)
    vbv = vbv.reshape(b, s, B_KV_HEADS, B_HEAD_DIM)
    if ctx is None:
        oa = diff_attention(qa, ka, va, lam)
        ob = sink_attention(qbq, kbk, vbv, sink)
        new_ctx = (ka.reshape(b, s, A_HEADS, 2 * A_QK_DIM), va, kbk, vbv)
    else:
        cos_a, sin_a = rope_a
        cos_b, sin_b = rope_b
        kc_a, vc_a, kc_b, vc_b = ctx
        n_ctx = kc_a.shape[1]
        k_all = jnp.concatenate([apply_rope(ka, cos_a, sin_a),
                                 kc_a.reshape(b, n_ctx, A_HEADS, 2, A_QK_DIM)], axis=1)
        v_all = jnp.concatenate([va, vc_a], axis=1)
        oa = diff_attention(apply_rope(qa, cos_a, sin_a), k_all, v_all, lam)
        ob = banded_sink_attention(apply_rope(qbq, cos_b, sin_b), apply_rope(kbk, cos_b, sin_b),
                                   vbv, kc_b, vc_b, sink)
        new_ctx = None
    oa = (rms_norm(oa, diff_g) * (1.0 - lam_init)).reshape(b, s, A_WIDTH)
    oc = fourier_mix(uc, w_f)
    mix = jnp.concatenate([oa, ob, oc], axis=-1) * jax.nn.silu(z)
    return x + gate * (mix @ w_out), new_ctx


def setup_inputs(seed: int = 0) -> dict:
    key = jax.random.key(seed)
    ks = jax.random.split(key, 24)
    f32 = jnp.float32
    n = lambda k, shp: jax.random.normal(k, shp, dtype=f32)
    return {
        "x_prompt": n(ks[0], (BATCH, SEQ, D_MODEL)),
        "x_sample": n(ks[1], (DEC_BATCH, DEC_SEQ, D_MODEL)),
        "cache_diff_k": n(ks[2], (DEC_BATCH, DEPTH, PAST_LEN, A_HEADS, 2 * A_QK_DIM)),
        "cache_diff_v": n(ks[3], (DEC_BATCH, DEPTH, PAST_LEN, A_HEADS, A_V_DIM)),
        "cache_swa_k": n(ks[4], (DEC_BATCH, DEPTH, PAST_LEN, B_KV_HEADS, B_HEAD_DIM)),
        "cache_swa_v": n(ks[5], (DEC_BATCH, DEPTH, PAST_LEN, B_KV_HEADS, B_HEAD_DIM)),
        "c": n(ks[6], (DEC_BATCH, D_MODEL)),
        "c_ctx": n(ks[7], (D_MODEL,)),
        "w_ada": n(ks[8], (DEPTH, D_MODEL, 3 * D_MODEL)) * D_MODEL ** -0.5,
        "b_ada": n(ks[9], (DEPTH, 3 * D_MODEL)) * 0.01,
        "norm_g": 1.0 + 0.1 * n(ks[10], (DEPTH, D_MODEL)),
        "w_in": n(ks[11], (DEPTH, D_MODEL, IN_COLS)) * D_MODEL ** -0.5,
        "lam_q1": n(ks[12], (DEPTH, A_QK_DIM)) * 0.1,
        "lam_k1": n(ks[13], (DEPTH, A_QK_DIM)) * 0.1,
        "lam_q2": n(ks[14], (DEPTH, A_QK_DIM)) * 0.1,
        "lam_k2": n(ks[15], (DEPTH, A_QK_DIM)) * 0.1,
        "diff_norm_g": 1.0 + 0.1 * n(ks[16], (DEPTH, A_V_DIM)),
        "sink": n(ks[17], (DEPTH, B_KV_HEADS, B_GROUP)) * 0.5,
        "w_fourier": n(ks[18], (DEPTH, C_GROUPS, C_GROUP_DIM, C_GROUP_DIM)) * C_GROUP_DIM ** -0.5,
        "w_out": n(ks[19], (DEPTH, D_MIX, D_MODEL)) * D_MIX ** -0.5,
        "final_g": 1.0 + 0.1 * n(ks[20], (D_MODEL,)),
    }


def reference(x_prompt, x_sample, cache_diff_k, cache_diff_v, cache_swa_k, cache_swa_v, c, c_ctx,
              w_ada, b_ada, norm_g, w_in, lam_q1, lam_k1, lam_q2, lam_k2, diff_norm_g, sink,
              w_fourier, w_out, final_g):
    n_lat = x_sample.shape[1]
    rope_a = axial_rope_tables(n_lat, A_QK_DIM, x_sample.dtype)
    rope_b = axial_rope_tables(n_lat, B_HEAD_DIM, x_sample.dtype)
    xc = x_prompt
    xs = x_sample
    new_dk, new_dv, new_sk, new_sv = [], [], [], []
    for l in range(DEPTH):
        lam_init = 0.8 - 0.6 * math.exp(-0.3 * l)
        lam = (jnp.exp(jnp.sum(lam_q1[l].astype(jnp.float32) * lam_k1[l].astype(jnp.float32)))
               - jnp.exp(jnp.sum(lam_q2[l].astype(jnp.float32) * lam_k2[l].astype(jnp.float32)))
               + lam_init)
        mod_c = jax.nn.silu(c_ctx) @ w_ada[l] + b_ada[l]
        sh_c, sc_c, g_c = jnp.split(mod_c, 3, axis=-1)
        xc, (dk, dv, sk, sv) = mixer_layer(xc, sh_c, sc_c, g_c, norm_g[l], w_in[l], lam, lam_init,
                                           diff_norm_g[l], sink[l], w_fourier[l], w_out[l])
        new_dk.append(dk)
        new_dv.append(dv)
        new_sk.append(sk)
        new_sv.append(sv)
        mod_s = (jax.nn.silu(c) @ w_ada[l] + b_ada[l])[:, None, :]
        sh_s, sc_s, g_s = jnp.split(mod_s, 3, axis=-1)
        ctx = (cache_diff_k[:, l], cache_diff_v[:, l], cache_swa_k[:, l], cache_swa_v[:, l])
        xs, _ = mixer_layer(xs, sh_s, sc_s, g_s, norm_g[l], w_in[l], lam, lam_init,
                            diff_norm_g[l], sink[l], w_fourier[l], w_out[l],
                            rope_a=rope_a, rope_b=rope_b, ctx=ctx)
    y_prompt = rms_norm(xc, final_g)
    y_sample = rms_norm(xs, final_g)
    new_diff_k = jnp.stack(new_dk, axis=1)
    new_diff_v = jnp.stack(new_dv, axis=1)
    new_swa_k = jnp.stack(new_sk, axis=1)
    new_swa_v = jnp.stack(new_sv, axis=1)
    return (y_prompt, y_sample, new_diff_k, new_diff_v, new_swa_k, new_swa_v)
```

```python
import functools
import math

import numpy as np
import jax
import jax.numpy as jnp
from jax import lax
from jax.experimental import pallas as pl
from jax.experimental.pallas import tpu as pltpu

D_MODEL = 1024
DEPTH = 4
GRID_W = 64
WINDOW = 128
ROPE_BASE = 10000.0
EPS = 1e-6
NEG_INF = -1e30
A_HEADS = 4
A_V_DIM = 128
A_QK_DIM = 64
B_KV_HEADS = 2
B_GROUP = 2
B_HEAD_DIM = 64
C_GROUPS = 4
C_GROUP_DIM = 64
C_WIDTH = 256

_QA0, _KA0, _VA0, _QB0, _KB0, _VB0, _UC0, _Z0, _IN_COLS = (
    0, 512, 1024, 1536, 1792, 1920, 2048, 2304, 3328)

LANES = 128
V7X_VMEM_BYTES = 64 * 1024 * 1024
LOG2E = 1.4426950408889634
Q_SCALE = (A_QK_DIM ** -0.5) * LOG2E

_NT = (((1,), (1,)), ((), ()))

f32 = jnp.float32
bf16 = jnp.bfloat16


def _vmem_limit(nbytes):
    return int(min(V7X_VMEM_BYTES - (6 << 20), max(32 << 20, nbytes)))


def _mod_kernel(c_ref, w_ref, b_ref, o_ref):
    cv = c_ref[...]
    s = cv * (1.0 / (1.0 + jnp.exp(-cv)))
    o_ref[0] = jnp.dot(s.astype(bf16), w_ref[0].astype(bf16),
                       preferred_element_type=f32) + b_ref[0]


def _modulation(cvec, w_ada, b_ada):
    rows = cvec.shape[0]
    ncol = 3 * D_MODEL
    tn = 1024
    return pl.pallas_call(
        _mod_kernel,
        grid=(DEPTH, ncol // tn),
        in_specs=[
            pl.BlockSpec((rows, D_MODEL), lambda l, j: (0, 0)),
            pl.BlockSpec((1, D_MODEL, tn), lambda l, j: (l, 0, j)),
            pl.BlockSpec((1, 1, tn), lambda l, j: (l, 0, j)),
        ],
        out_specs=pl.BlockSpec((1, rows, tn), lambda l, j: (l, 0, j)),
        out_shape=jax.ShapeDtypeStruct((DEPTH, rows, ncol), f32),
        name="modulation",
    )(cvec, w_ada, b_ada.reshape(DEPTH, 1, ncol))


def _fw_kernel(cc_ref, sc_ref, w_ref, o_ref, *, norms):
    w = w_ref[0]
    m1 = jnp.dot(cc_ref[...], w, preferred_element_type=f32, precision=lax.Precision.HIGHEST)
    m2 = jnp.dot(sc_ref[...], w, preferred_element_type=f32, precision=lax.Precision.HIGHEST)
    for i, nrm in enumerate(norms):
        o_ref[0, i, :, 0:C_WIDTH] = (m1 * nrm).astype(o_ref.dtype)
        o_ref[0, i, :, C_WIDTH:2 * C_WIDTH] = (m2 * nrm).astype(o_ref.dtype)


def _fourier_weights(w_fourier, seq_lens):
    eye = jnp.eye(C_GROUPS, dtype=f32)
    wbd = jnp.einsum('lgcd,gh->lgchd', w_fourier.astype(f32), eye).reshape(DEPTH, C_WIDTH, C_WIDTH)
    ch = np.arange(C_GROUP_DIM)
    ang = 2.0 * np.pi * ((ch[:, None] * ch[None, :]) % C_GROUP_DIM) / C_GROUP_DIM
    bd = np.kron(np.eye(C_GROUPS), np.ones((C_GROUP_DIM, C_GROUP_DIM)))
    cc = jnp.asarray(np.tile(np.cos(ang), (C_GROUPS, C_GROUPS)) * bd, f32)
    sc = jnp.asarray(np.tile(np.sin(ang), (C_GROUPS, C_GROUPS)) * bd, f32)
    norms = tuple(1.0 / math.sqrt(s * C_GROUP_DIM) for s in seq_lens)
    return pl.pallas_call(
        functools.partial(_fw_kernel, norms=norms),
        grid=(DEPTH,),
        in_specs=[
            pl.BlockSpec((C_WIDTH, C_WIDTH), lambda l: (0, 0)),
            pl.BlockSpec((C_WIDTH, C_WIDTH), lambda l: (0, 0)),
            pl.BlockSpec((1, C_WIDTH, C_WIDTH), lambda l: (l, 0, 0)),
        ],
        out_specs=pl.BlockSpec((1, len(norms), C_WIDTH, 2 * C_WIDTH), lambda l: (l, 0, 0, 0)),
        out_shape=jax.ShapeDtypeStruct((DEPTH, len(norms), C_WIDTH, 2 * C_WIDTH), bf16),
        name="fourier_weights",
    )(cc, sc, wbd)


def _dft_table(s):
    n = np.arange(s)
    ang = 2.0 * np.pi * ((n[:, None] * n[None, :]) % s) / s
    tab = np.concatenate([np.cos(ang), -np.sin(ang)], axis=1).astype(np.float32)
    return jnp.asarray(tab).astype(bf16)


def _rope_tables(n_tokens):
    dim = A_QK_DIM
    rows = n_tokens // GRID_W
    row = jnp.repeat(jnp.arange(rows), GRID_W).astype(f32)
    col = jnp.tile(jnp.arange(GRID_W), rows).astype(f32)
    nf = dim // 4
    inv = ROPE_BASE ** (-jnp.arange(nf, dtype=f32) / nf)
    ar = row[:, None] * inv
    ac = col[:, None] * inv
    ang = jnp.concatenate([ar, ar, ac, ac], axis=-1)
    cos = jnp.tile(jnp.cos(ang), (1, LANES // dim))
    sin = jnp.tile(jnp.sin(ang), (1, LANES // dim))
    second = (np.arange(LANES) % (2 * nf)) >= nf
    sin_p = jnp.where(second[None, :], sin, 0.0)
    sin_n = jnp.where(second[None, :], 0.0, -sin)
    return cos.astype(f32), sin_p.astype(f32), sin_n.astype(f32)


def _in_proj_kernel(*refs, rope, emit_cache):
    it = iter(refs)
    x_ref, mod_ref, g_ref, w_ref = next(it), next(it), next(it), next(it)
    if rope:
        cos_ref, sp_ref, sn_ref = next(it), next(it), next(it)
    qa_ref, ka_ref, va_ref, qb_ref, kb_ref, vb_ref, uc_ref, sz_ref = (next(it) for _ in range(8))
    if emit_cache:
        ka32_ref, va32_ref, kb32_ref, vb32_ref = (next(it) for _ in range(4))

    x = x_ref[0]
    ms = jnp.mean(x * x, axis=-1, keepdims=True)
    y = x * lax.rsqrt(ms + EPS) * g_ref[...]
    shift = mod_ref[0, :, 0:D_MODEL]
    scale = mod_ref[0, :, D_MODEL:2 * D_MODEL]
    hb = (y * (1.0 + scale) + shift).astype(bf16)

    def proj(c0, c1):
        return jnp.dot(hb, w_ref[:, c0:c1], preferred_element_type=f32)

    lane = lax.broadcasted_iota(jnp.int32, (x.shape[0], LANES), 1)
    lo = lane < (LANES // 2)

    if rope:
        cos, sin_p, sin_n = cos_ref[...], sp_ref[...], sn_ref[...]
        nf = A_QK_DIM // 4

        def rot(t):
            return (t * cos + pltpu.roll(t, nf, 1) * sin_p
                    + pltpu.roll(t, LANES - nf, 1) * sin_n)
    else:
        def rot(t):
            return t

    qa = proj(_QA0, _KA0)
    for h in range(A_HEADS):
        t = rot(qa[:, h * LANES:(h + 1) * LANES]) * Q_SCALE
        qa_ref[0, :, (2 * h) * LANES:(2 * h + 1) * LANES] = jnp.where(lo, t, 0.0).astype(bf16)
        qa_ref[0, :, (2 * h + 1) * LANES:(2 * h + 2) * LANES] = jnp.where(lo, 0.0, t).astype(bf16)

    ka = proj(_KA0, _VA0)
    if emit_cache:
        ka32_ref[0] = ka
    for h in range(A_HEADS):
        ka_ref[0, :, h * LANES:(h + 1) * LANES] = rot(ka[:, h * LANES:(h + 1) * LANES]).astype(bf16)

    va = proj(_VA0, _QB0)
    if emit_cache:
        va32_ref[0] = va
    va_ref[0] = va.astype(bf16)

    pb = proj(_QB0, _UC0)
    for kvh in range(B_KV_HEADS):
        t = rot(pb[:, kvh * LANES:(kvh + 1) * LANES]) * Q_SCALE
        tr = pltpu.roll(t, LANES // 2, 1)
        if kvh == 0:
            e0, e1 = jnp.where(lo, t, 0.0), jnp.where(lo, tr, 0.0)
        else:
            e0, e1 = jnp.where(lo, 0.0, tr), jnp.where(lo, 0.0, t)
        qb_ref[0, :, (2 * kvh) * LANES:(2 * kvh + 1) * LANES] = e0.astype(bf16)
        qb_ref[0, :, (2 * kvh + 1) * LANES:(2 * kvh + 2) * LANES] = e1.astype(bf16)
    kb = pb[:, 2 * LANES:3 * LANES]
    vb = pb[:, 3 * LANES:4 * LANES]
    if emit_cache:
        kb32_ref[0] = kb
        vb32_ref[0] = vb
    kb_ref[0] = rot(kb).astype(bf16)
    vbr = pltpu.roll(vb, LANES // 2, 1)
    vb_ref[0, :, 0:LANES] = jnp.where(lo, vb, vbr).astype(bf16)
    vb_ref[0, :, LANES:2 * LANES] = jnp.where(lo, vbr, vb).astype(bf16)

    uc_ref[0] = proj(_UC0, _Z0).astype(bf16)
    z = proj(_Z0, _IN_COLS)
    sz_ref[0] = (z * (1.0 / (1.0 + jnp.exp(-z)))).astype(bf16)


def _in_proj(x, mod, norm_g, w_bf, rope_tabs, tm, emit_cache):
    b, s, _ = x.shape
    rope = rope_tabs is not None
    grid = (b, s // tm)
    tok = lambda width: pl.BlockSpec((1, tm, width), lambda i, j: (i, j, 0))
    mod_idx = (lambda i, j: (i, 0, 0)) if mod.shape[0] == b else (lambda i, j: (0, 0, 0))
    in_specs = [
        tok(D_MODEL),
        pl.BlockSpec((1, 1, 3 * D_MODEL), mod_idx),
        pl.BlockSpec((1, D_MODEL), lambda i, j: (0, 0)),
        pl.BlockSpec((D_MODEL, _IN_COLS), lambda i, j: (0, 0)),
    ]
    args = [x, mod, norm_g.reshape(1, D_MODEL), w_bf]
    if rope:
        in_specs += [pl.BlockSpec((tm, LANES), lambda i, j: (j, 0))] * 3
        args += list(rope_tabs)
    widths = [1024, 512, 512, 512, 128, 256, 256, 1024]
    out_specs = [tok(w) for w in widths]
    out_shape = [jax.ShapeDtypeStruct((b, s, w), bf16) for w in widths]
    if emit_cache:
        cw = [512, 512, 128, 128]
        out_specs += [tok(w) for w in cw]
        out_shape += [jax.ShapeDtypeStruct((b, s, w), f32) for w in cw]
    vmem = (2 * tm * D_MODEL * 4 + 2 * D_MODEL * _IN_COLS * 2 + 2 * tm * sum(widths) * 2
            + 6 * tm * LANES * 4 + tm * _IN_COLS * 4 * 2 + (8 << 20))
    return pl.pallas_call(
        functools.partial(_in_proj_kernel, rope=rope, emit_cache=emit_cache),
        grid=grid,
        in_specs=in_specs,
        out_specs=out_specs,
        out_shape=out_shape,
        compiler_params=pltpu.CompilerParams(
            dimension_semantics=("arbitrary", "arbitrary"),
            vmem_limit_bytes=_vmem_limit(vmem)),
        name="in_proj_lat" if rope else "in_proj_ctx",
    )(*args)


def _attend(q, pieces, extra=None):
    scores = []
    for k, _, mask in pieces:
        s = lax.dot_general(q, k, _NT, preferred_element_type=f32)
        if mask is not None:
            s = jnp.where(mask, s, NEG_INF)
        scores.append(s)
    m = jnp.max(scores[0], axis=-1, keepdims=True)
    for s in scores[1:]:
        m = jnp.maximum(m, jnp.max(s, axis=-1, keepdims=True))
    if extra is not None:
        m = jnp.maximum(m, extra)
    l = None
    o = None
    for s, (_, v, _) in zip(scores, pieces):
        e = jnp.exp2(s - m)
        ls = jnp.sum(e, axis=-1, keepdims=True)
        os = jnp.dot(e.astype(bf16), v, preferred_element_type=f32)
        l = ls if l is None else l + ls
        o = os if o is None else o + os
    if extra is not None:
        l = l + jnp.exp2(extra - m)
    return o * (1.0 / l)


ATTN_LOOKAHEAD = 3


def _attend_t(streams, finish):
    groups = []
    for tag, _, _ in streams:
        if tag[0] not in groups:
            groups.append(tag[0])
    work = []
    for g in groups:
        members = [st for st in streams if st[0][0] == g]
        for pi in range(len(members[0][2])):
            for tag, q_load, pieces in members:
                work.append((tag, q_load, pieces[pi], pi == len(pieces) - 1))
    n = len(work)
    m, l, o, scores = {}, {}, {}, {}

    def issue_scores(i):
        tag, q_load, (k_load, _), _ = work[i]
        scores[i] = lax.dot_general(k_load(), q_load(), _NT, preferred_element_type=f32)

    def consume(i):
        tag, _, (_, vt_load), last = work[i]
        s = scores.pop(i)
        mj = jnp.max(s, axis=0, keepdims=True)
        if tag not in m:
            m[tag] = mj
            e = jnp.exp2(s - mj)
            l[tag] = jnp.sum(e, axis=0, keepdims=True)
            o[tag] = jnp.dot(vt_load(), e.astype(bf16), preferred_element_type=f32)
        else:
            mn = jnp.maximum(m[tag], mj)
            alpha = jnp.exp2(m[tag] - mn)
            e = jnp.exp2(s - mn)
            l[tag] = l[tag] * alpha + jnp.sum(e, axis=0, keepdims=True)
            o[tag] = o[tag] * alpha + jnp.dot(vt_load(), e.astype(bf16),
                                              preferred_element_type=f32)
            m[tag] = mn
        if last:
            finish(tag, o.pop(tag) * (1.0 / l.pop(tag)))

    for i in range(min(ATTN_LOOKAHEAD, n)):
        issue_scores(i)
    for i in range(n):
        if i + ATTN_LOOKAHEAD < n:
            issue_scores(i + ATTN_LOOKAHEAD)
        consume(i)


def _mixer_kernel(*refs, latent, tq, seq, kchunk, lam_init, final):
    it = iter(refs)
    x_ref, qa_ref, ka_ref, va_ref = (next(it) for _ in range(4))
    if latent:
        kca_ref, vca_ref = next(it), next(it)
    qb_ref, kb_ref, vb_ref = (next(it) for _ in range(3))
    if latent:
        kcb_ref, vcb_ref = next(it), next(it)
    uc_ref, sz_ref, dtab_ref, mcat_ref, wout_ref, mod_ref = (next(it) for _ in range(6))
    lamv_ref, dg_ref, sink_ref = (next(it) for _ in range(3))
    if final:
        fg_ref = next(it)
    out_ref = next(it)
    tcat_ref, mix_ref, vt_ref = next(it), next(it), next(it)

    qi = pl.program_id(1)

    @pl.when(qi == 0)
    def _():
        t = jnp.dot(uc_ref[0], mcat_ref[0, 0], preferred_element_type=f32)
        tcat_ref[0:seq, :] = t[:, 0:C_WIDTH].astype(bf16)
        tcat_ref[seq:2 * seq, :] = t[:, C_WIDTH:2 * C_WIDTH].astype(bf16)
        vt_ref[:, 0:seq] = va_ref[0].T
        if latent:
            vt_ref[:, seq:] = vca_ref[0, 0].T

    lv = lamv_ref[...]
    lam = (jnp.exp(jnp.sum(lv[0:1] * lv[1:2], axis=-1, keepdims=True))
           - jnp.exp(jnp.sum(lv[2:3] * lv[3:4], axis=-1, keepdims=True)) + lam_init)

    dg = dg_ref[...]
    key_starts = list(range(0, seq, kchunk))

    def a_pieces(h):
        hs = slice(h * LANES, (h + 1) * LANES)
        ps = [(lambda j=j: ka_ref[0, j:j + kchunk, hs], lambda j=j: vt_ref[hs, j:j + kchunk])
              for j in key_starts]
        if latent:
            ps.append((lambda: kca_ref[0, 0, :, hs], lambda: vt_ref[hs, seq:]))
        return ps

    streams = [((h, c), (lambda h=h, c=c: qa_ref[0, :, (2 * h + c) * LANES:(2 * h + c + 1) * LANES]),
                a_pieces(h)) for h in range(A_HEADS) for c in range(2)]
    first = {}

    def a_finish(tag, out):
        h, c = tag
        if c == 0:
            first[h] = out
            return
        o = (first.pop(h) - lam * out).T
        on = o * lax.rsqrt(jnp.mean(o * o, axis=-1, keepdims=True) + EPS) * dg
        mix_ref[:, h * LANES:(h + 1) * LANES] = on * (1.0 - lam_init)

    _attend_t(streams, a_finish)

    lane = lax.broadcasted_iota(jnp.int32, (tq, LANES), 1)
    lo = lane < (LANES // 2)
    if latent:
        win = tq + 2 * WINDOW
        t0 = qi * tq
        ws = pl.multiple_of(jnp.clip(t0 - WINDOW, 0, seq - win), WINDOW)
        qpos = t0 + lax.broadcasted_iota(jnp.int32, (tq, win), 0)
        kpos = ws + lax.broadcasted_iota(jnp.int32, (tq, win), 1)
        band = jnp.abs(kpos - qpos) <= WINDOW
    for kvh in range(B_KV_HEADS):
        vs = slice(kvh * LANES, (kvh + 1) * LANES)
        if latent:
            pieces = [(kb_ref[0, pl.ds(ws, win), :], vb_ref[0, pl.ds(ws, win), vs], band),
                      (kcb_ref[0, 0], vcb_ref[0, 0, :, vs], None)]
        else:
            pieces = [(kb_ref[0], vb_ref[0, :, vs], None)]
        og = []
        for g in range(B_GROUP):
            idx = kvh * B_GROUP + g
            sk = jnp.full((1, 1), sink_ref[idx] * LOG2E, f32)
            og.append(_attend(qb_ref[0, :, idx * LANES:(idx + 1) * LANES], pieces, extra=sk))
        mix_ref[:, 512 + kvh * LANES:512 + (kvh + 1) * LANES] = jnp.where(lo, og[0], og[1])

    mix_ref[:, 768:1024] = jnp.dot(dtab_ref[...], tcat_ref[...], preferred_element_type=f32)

    mixed = (mix_ref[...] * sz_ref[0].astype(f32)).astype(bf16)
    y = jnp.dot(mixed, wout_ref[...], preferred_element_type=f32)
    gate = mod_ref[0, :, 2 * D_MODEL:3 * D_MODEL]
    xn = x_ref[0] + gate * y
    if final:
        xn = xn * lax.rsqrt(jnp.mean(xn * xn, axis=-1, keepdims=True) + EPS) * fg_ref[...]
    out_ref[0] = xn


def _mixer(x, proj, cache, mcat_l, path_idx, dtab, wout_bf, mod, lamv, dg, sink_l, lam_init,
           final_g, tq, layer):
    b, s, _ = x.shape
    latent = cache is not None
    final = final_g is not None
    qa, ka, va, qb, kb, vb, uc, sz = proj[:8]
    grid = (b, s // tq)
    qtile = lambda width: pl.BlockSpec((1, tq, width), lambda i, j: (i, j, 0))
    full = lambda width: pl.BlockSpec((1, s, width), lambda i, j: (i, 0, 0))
    const2 = lambda shape: pl.BlockSpec(shape, lambda i, j: (0, 0))
    mod_idx = (lambda i, j: (i, 0, 0)) if mod.shape[0] == b else (lambda i, j: (0, 0, 0))

    in_specs = [qtile(D_MODEL), qtile(1024), full(512), full(512)]
    args = [x, qa, ka, va]
    if latent:
        kca, vca, kcb, vcb = cache
        n_ctx = kca.shape[2]
        cspec = lambda width: pl.BlockSpec((1, 1, n_ctx, width), lambda i, j: (i, layer, 0, 0))
        in_specs += [cspec(512), cspec(512)]
        args += [kca, vca]
    in_specs += [qtile(512), full(128), full(256)]
    args += [qb, kb, vb]
    if latent:
        in_specs += [cspec(128), cspec(256)]
        args += [kcb, vcb]
    in_specs += [
        full(256), qtile(1024),
        pl.BlockSpec((tq, 2 * s), lambda i, j: (j, 0)),
        pl.BlockSpec((1, 1, C_WIDTH, 2 * C_WIDTH), lambda i, j: (layer, path_idx, 0, 0)),
        const2((D_MODEL, D_MODEL)),
        pl.BlockSpec((1, 1, 3 * D_MODEL), mod_idx),
        const2((4, A_QK_DIM)), const2((1, A_V_DIM)),
        pl.BlockSpec(memory_space=pltpu.SMEM),
    ]
    args += [uc, sz, dtab, mcat_l, wout_bf, mod, lamv, dg.reshape(1, A_V_DIM), sink_l]
    if final:
        in_specs.append(const2((1, D_MODEL)))
        args.append(final_g.reshape(1, D_MODEL))

    nk = s + (256 if latent else 0)
    vmem = (2 * (2 * tq * D_MODEL * 4 + tq * 2560 * 2 + s * (512 + 512 + 128 + 256 + 256) * 2
                 + tq * 2 * s * 2 + D_MODEL * D_MODEL * 2)
            + 2 * s * C_WIDTH * 2 + tq * D_MODEL * 4 + 6 * tq * nk * 4 + (8 << 20))
    return pl.pallas_call(
        functools.partial(_mixer_kernel, latent=latent, tq=tq, seq=s, kchunk=min(s, 512),
                          lam_init=lam_init, final=final),
        grid=grid,
        in_specs=in_specs,
        out_specs=qtile(D_MODEL),
        out_shape=jax.ShapeDtypeStruct((b, s, D_MODEL), f32),
        scratch_shapes=[pltpu.VMEM((2 * s, C_WIDTH), bf16), pltpu.VMEM((tq, D_MODEL), f32),
                        pltpu.VMEM((512, nk), bf16)],
        compiler_params=pltpu.CompilerParams(
            dimension_semantics=("arbitrary", "arbitrary"),
            vmem_limit_bytes=_vmem_limit(vmem)),
        name="mixer_lat" if latent else "mixer_ctx",
    )(*args)


def kernel(x_prompt, x_sample, cache_diff_k, cache_diff_v, cache_swa_k, cache_swa_v, c, c_ctx,
           w_ada, b_ada, norm_g, w_in, lam_q1, lam_k1, lam_q2, lam_k2, diff_norm_g, sink,
           w_fourier, w_out, final_g):
    bc, sc_len, _ = x_prompt.shape
    bs, ss_len, _ = x_sample.shape
    n_ctx = cache_diff_k.shape[2]

    rows = 16
    cvec = jnp.zeros((rows, D_MODEL), f32).at[0:bs].set(c).at[bs].set(c_ctx)
    mod = _modulation(cvec, w_ada, b_ada)
    mcat = _fourier_weights(w_fourier, (ss_len, sc_len))

    w_in_bf = w_in.astype(bf16)
    w_out_bf = w_out.astype(bf16)
    rope_tabs = _rope_tables(ss_len)
    dtab_s = _dft_table(ss_len)
    dtab_c = _dft_table(sc_len)

    kca = cache_diff_k.reshape(bs, DEPTH, n_ctx, 512).astype(bf16)
    vca = cache_diff_v.reshape(bs, DEPTH, n_ctx, 512).astype(bf16)
    kcb = cache_swa_k.reshape(bs, DEPTH, n_ctx, 128).astype(bf16)
    vcb = jnp.repeat(cache_swa_v, 2, axis=3).reshape(bs, DEPTH, n_ctx, 256).astype(bf16)

    lamv = jnp.stack([lam_q1, lam_k1, lam_q2, lam_k2], axis=1).astype(f32)
    sink_flat = sink.reshape(DEPTH, B_KV_HEADS * B_GROUP).astype(f32)

    xc, xs = x_prompt, x_sample
    new_dk, new_dv, new_sk, new_sv = [], [], [], []
    for l in range(DEPTH):
        lam_init = 0.8 - 0.6 * math.exp(-0.3 * l)
        last = l == DEPTH - 1
        mod_c = mod[l, bs:bs + 1][None]
        mod_s = mod[l, 0:bs][:, None, :]

        pc = _in_proj(xc, mod_c, norm_g[l], w_in_bf[l], None, 256, True)
        new_dk.append(pc[8].reshape(bc, sc_len, A_HEADS, 2 * A_QK_DIM))
        new_dv.append(pc[9].reshape(bc, sc_len, A_HEADS, A_V_DIM))
        new_sk.append(pc[10].reshape(bc, sc_len, B_KV_HEADS, B_HEAD_DIM))
        new_sv.append(pc[11].reshape(bc, sc_len, B_KV_HEADS, B_HEAD_DIM))
        xc = _mixer(xc, pc, None, mcat, 1, dtab_c, w_out_bf[l], mod_c, lamv[l], diff_norm_g[l],
                    sink_flat[l], lam_init, final_g if last else None, 256, l)

        ps = _in_proj(xs, mod_s, norm_g[l], w_in_bf[l], rope_tabs, 512, False)
        xs = _mixer(xs, ps, (kca, vca, kcb, vcb), mcat, 0, dtab_s, w_out_bf[l], mod_s, lamv[l],
                    diff_norm_g[l], sink_flat[l], lam_init, final_g if last else None, 256, l)

    return (xc, xs, jnp.stack(new_dk, axis=1), jnp.stack(new_dv, axis=1),
            jnp.stack(new_sk, axis=1), jnp.stack(new_sv, axis=1))
```

```python
import functools
import math

import numpy as np
import jax
import jax.numpy as jnp
from jax import lax
from jax.experimental import pallas as pl
from jax.experimental.pallas import tpu as pltpu

D_MODEL = 1024
DEPTH = 4
GRID_W = 64
WINDOW = 128
ROPE_BASE = 10000.0
EPS = 1e-6
NEG_INF = -1e30
A_HEADS = 4
A_V_DIM = 128
A_QK_DIM = 64
B_KV_HEADS = 2
B_GROUP = 2
B_HEAD_DIM = 64
C_GROUPS = 4
C_GROUP_DIM = 64
C_WIDTH = 256

_QA0, _KA0, _VA0, _QB0, _KB0, _VB0, _UC0, _Z0, _IN_COLS = (
    0, 512, 1024, 1536, 1792, 1920, 2048, 2304, 3328)

LANES = 128
V7X_VMEM_BYTES = 64 * 1024 * 1024
LOG2E = 1.4426950408889634
Q_SCALE = (A_QK_DIM ** -0.5) * LOG2E

_NT = (((1,), (1,)), ((), ()))

f32 = jnp.float32
bf16 = jnp.bfloat16


def _vmem_limit(nbytes):
    return int(min(V7X_VMEM_BYTES - (6 << 20), max(32 << 20, nbytes)))


def _mod_kernel(c_ref, w_ref, b_ref, o_ref):
    cv = c_ref[...]
    s = cv * (1.0 / (1.0 + jnp.exp(-cv)))
    o_ref[0] = jnp.dot(s.astype(bf16), w_ref[0].astype(bf16),
                       preferred_element_type=f32) + b_ref[0]


def _modulation(cvec, w_ada, b_ada):
    rows = cvec.shape[0]
    ncol = 3 * D_MODEL
    tn = 1024
    return pl.pallas_call(
        _mod_kernel,
        grid=(DEPTH, ncol // tn),
        in_specs=[
            pl.BlockSpec((rows, D_MODEL), lambda l, j: (0, 0)),
            pl.BlockSpec((1, D_MODEL, tn), lambda l, j: (l, 0, j)),
            pl.BlockSpec((1, 1, tn), lambda l, j: (l, 0, j)),
        ],
        out_specs=pl.BlockSpec((1, rows, tn), lambda l, j: (l, 0, j)),
        out_shape=jax.ShapeDtypeStruct((DEPTH, rows, ncol), f32),
        name="modulation",
    )(cvec, w_ada, b_ada.reshape(DEPTH, 1, ncol))


def _fw_kernel(cc_ref, sc_ref, w_ref, o_ref, *, norms):
    w = w_ref[0]
    m1 = jnp.dot(cc_ref[...], w, preferred_element_type=f32, precision=lax.Precision.HIGHEST)
    m2 = jnp.dot(sc_ref[...], w, preferred_element_type=f32, precision=lax.Precision.HIGHEST)
    for i, nrm in enumerate(norms):
        o_ref[0, i, :, 0:C_WIDTH] = (m1 * nrm).astype(o_ref.dtype)
        o_ref[0, i, :, C_WIDTH:2 * C_WIDTH] = (m2 * nrm).astype(o_ref.dtype)


def _fourier_weights(w_fourier, seq_lens):
    eye = jnp.eye(C_GROUPS, dtype=f32)
    wbd = jnp.einsum('lgcd,gh->lgchd', w_fourier.astype(f32), eye).reshape(DEPTH, C_WIDTH, C_WIDTH)
    ch = np.arange(C_GROUP_DIM)
    ang = 2.0 * np.pi * ((ch[:, None] * ch[None, :]) % C_GROUP_DIM) / C_GROUP_DIM
    bd = np.kron(np.eye(C_GROUPS), np.ones((C_GROUP_DIM, C_GROUP_DIM)))
    cc = jnp.asarray(np.tile(np.cos(ang), (C_GROUPS, C_GROUPS)) * bd, f32)
    sc = jnp.asarray(np.tile(np.sin(ang), (C_GROUPS, C_GROUPS)) * bd, f32)
    norms = tuple(1.0 / math.sqrt(s * C_GROUP_DIM) for s in seq_lens)
    return pl.pallas_call(
        functools.partial(_fw_kernel, norms=norms),
        grid=(DEPTH,),
        in_specs=[
            pl.BlockSpec((C_WIDTH, C_WIDTH), lambda l: (0, 0)),
            pl.BlockSpec((C_WIDTH, C_WIDTH), lambda l: (0, 0)),
            pl.BlockSpec((1, C_WIDTH, C_WIDTH), lambda l: (l, 0, 0)),
        ],
        out_specs=pl.BlockSpec((1, len(norms), C_WIDTH, 2 * C_WIDTH), lambda l: (l, 0, 0, 0)),
        out_shape=jax.ShapeDtypeStruct((DEPTH, len(norms), C_WIDTH, 2 * C_WIDTH), bf16),
        name="fourier_weights",
    )(cc, sc, wbd)


def _dft_table(s):
    n = np.arange(s)
    ang = 2.0 * np.pi * ((n[:, None] * n[None, :]) % s) / s
    tab = np.concatenate([np.cos(ang), -np.sin(ang)], axis=1).astype(np.float32)
    return jnp.asarray(tab).astype(bf16)


def _rope_tables(n_tokens):
    dim = A_QK_DIM
    rows = n_tokens // GRID_W
    row = jnp.repeat(jnp.arange(rows), GRID_W).astype(f32)
    col = jnp.tile(jnp.arange(GRID_W), rows).astype(f32)
    nf = dim // 4
    inv = ROPE_BASE ** (-jnp.arange(nf, dtype=f32) / nf)
    ar = row[:, None] * inv
    ac = col[:, None] * inv
    ang = jnp.concatenate([ar, ar, ac, ac], axis=-1)
    cos = jnp.tile(jnp.cos(ang), (1, LANES // dim))
    sin = jnp.tile(jnp.sin(ang), (1, LANES // dim))
    second = (np.arange(LANES) % (2 * nf)) >= nf
    sin_p = jnp.where(second[None, :], sin, 0.0)
    sin_n = jnp.where(second[None, :], 0.0, -sin)
    return cos.astype(f32), sin_p.astype(f32), sin_n.astype(f32)


def _in_proj_kernel(*refs, rope, emit_cache, n_alias):
    it = iter(refs)
    x_ref, mod_ref, g_ref, w_ref = next(it), next(it), next(it), next(it)
    if rope:
        cos_ref, sp_ref, sn_ref = next(it), next(it), next(it)
    for _ in range(n_alias):
        next(it)
    qa_ref, ka_ref, va_ref, qb_ref, kb_ref, vb_ref, uc_ref, sz_ref = (next(it) for _ in range(8))
    if emit_cache:
        dk_ref, dv_ref, sk_ref, sv_ref = (next(it) for _ in range(4))

    x = x_ref[0]
    ms = jnp.mean(x * x, axis=-1, keepdims=True)
    y = x * lax.rsqrt(ms + EPS) * g_ref[...]
    shift = mod_ref[0, :, 0:D_MODEL]
    scale = mod_ref[0, :, D_MODEL:2 * D_MODEL]
    hb = (y * (1.0 + scale) + shift).astype(bf16)

    def proj(c0, c1):
        return jnp.dot(hb, w_ref[:, c0:c1], preferred_element_type=f32)

    lane = lax.broadcasted_iota(jnp.int32, (x.shape[0], LANES), 1)
    lo = lane < (LANES // 2)

    if rope:
        cos, sin_p, sin_n = cos_ref[...], sp_ref[...], sn_ref[...]
        nf = A_QK_DIM // 4

        def rot(t):
            return (t * cos + pltpu.roll(t, nf, 1) * sin_p
                    + pltpu.roll(t, LANES - nf, 1) * sin_n)
    else:
        def rot(t):
            return t

    qa = proj(_QA0, _KA0)
    for h in range(A_HEADS):
        t = rot(qa[:, h * LANES:(h + 1) * LANES]) * Q_SCALE
        qa_ref[0, :, (2 * h) * LANES:(2 * h + 1) * LANES] = jnp.where(lo, t, 0.0).astype(bf16)
        qa_ref[0, :, (2 * h + 1) * LANES:(2 * h + 2) * LANES] = jnp.where(lo, 0.0, t).astype(bf16)

    ka = proj(_KA0, _VA0)
    if emit_cache:
        for h in range(A_HEADS):
            dk_ref[0, 0, :, h, :] = ka[:, h * LANES:(h + 1) * LANES]
    for h in range(A_HEADS):
        ka_ref[0, :, h * LANES:(h + 1) * LANES] = rot(ka[:, h * LANES:(h + 1) * LANES]).astype(bf16)

    va = proj(_VA0, _QB0)
    if emit_cache:
        for h in range(A_HEADS):
            dv_ref[0, 0, :, h, :] = va[:, h * LANES:(h + 1) * LANES]
    va_ref[0] = va.astype(bf16)

    pb = proj(_QB0, _UC0)
    for kvh in range(B_KV_HEADS):
        t = rot(pb[:, kvh * LANES:(kvh + 1) * LANES]) * Q_SCALE
        tr = pltpu.roll(t, LANES // 2, 1)
        if kvh == 0:
            e0, e1 = jnp.where(lo, t, 0.0), jnp.where(lo, tr, 0.0)
        else:
            e0, e1 = jnp.where(lo, 0.0, tr), jnp.where(lo, 0.0, t)
        qb_ref[0, :, (2 * kvh) * LANES:(2 * kvh + 1) * LANES] = e0.astype(bf16)
        qb_ref[0, :, (2 * kvh + 1) * LANES:(2 * kvh + 2) * LANES] = e1.astype(bf16)
    kb = pb[:, 2 * LANES:3 * LANES]
    vb = pb[:, 3 * LANES:4 * LANES]
    if emit_cache:
        sk_ref[0, 0] = kb
        sv_ref[0, 0] = vb
    kb_ref[0] = rot(kb).astype(bf16)
    vbr = pltpu.roll(vb, LANES // 2, 1)
    vb_ref[0, :, 0:LANES] = jnp.where(lo, vb, vbr).astype(bf16)
    vb_ref[0, :, LANES:2 * LANES] = jnp.where(lo, vbr, vb).astype(bf16)

    uc_ref[0] = proj(_UC0, _Z0).astype(bf16)
    z = proj(_Z0, _IN_COLS)
    sz_ref[0] = (z * (1.0 / (1.0 + jnp.exp(-z)))).astype(bf16)


def _in_proj(x, mod, norm_g, w_bf, rope_tabs, tm, cache_out=None):
    b, s, _ = x.shape
    rope = rope_tabs is not None
    emit_cache = cache_out is not None
    grid = (b, s // tm)
    tok = lambda width: pl.BlockSpec((1, tm, width), lambda i, j: (i, j, 0))
    mod_idx = (lambda i, j: (i, 0, 0)) if mod.shape[0] == b else (lambda i, j: (0, 0, 0))
    in_specs = [
        tok(D_MODEL),
        pl.BlockSpec((1, 1, 3 * D_MODEL), mod_idx),
        pl.BlockSpec((1, D_MODEL), lambda i, j: (0, 0)),
        pl.BlockSpec((D_MODEL, _IN_COLS), lambda i, j: (0, 0)),
    ]
    args = [x, mod, norm_g.reshape(1, D_MODEL), w_bf]
    if rope:
        in_specs += [pl.BlockSpec((tm, LANES), lambda i, j: (j, 0))] * 3
        args += list(rope_tabs)
    widths = [1024, 512, 512, 512, 128, 256, 256, 1024]
    out_specs = [tok(w) for w in widths]
    out_shape = [jax.ShapeDtypeStruct((b, s, w), bf16) for w in widths]
    aliases = {}
    n_alias = 0
    if emit_cache:
        layer, bufs = cache_out
        out_specs += [
            pl.BlockSpec((1, 1, tm, A_HEADS, LANES), lambda i, j: (i, layer, j, 0, 0)),
            pl.BlockSpec((1, 1, tm, A_HEADS, LANES), lambda i, j: (i, layer, j, 0, 0)),
            pl.BlockSpec((1, 1, tm, LANES), lambda i, j: (i, layer, j, 0)),
            pl.BlockSpec((1, 1, tm, LANES), lambda i, j: (i, layer, j, 0)),
        ]
        out_shape += [jax.ShapeDtypeStruct((b, DEPTH, s, A_HEADS, LANES), f32)] * 2
        out_shape += [jax.ShapeDtypeStruct((b, DEPTH, s, LANES), f32)] * 2
        if bufs is not None:
            n_alias = len(bufs)
            aliases = {len(args) + k: len(widths) + k for k in range(n_alias)}
            in_specs += [pl.BlockSpec(memory_space=pl.ANY)] * n_alias
            args += list(bufs)
    vmem = (2 * tm * D_MODEL * 4 + 2 * D_MODEL * _IN_COLS * 2 + 2 * tm * sum(widths) * 2
            + 6 * tm * LANES * 4 + tm * _IN_COLS * 4 * 2 + (8 << 20))
    return pl.pallas_call(
        functools.partial(_in_proj_kernel, rope=rope, emit_cache=emit_cache, n_alias=n_alias),
        grid=grid,
        in_specs=in_specs,
        out_specs=out_specs,
        out_shape=out_shape,
        input_output_aliases=aliases,
        compiler_params=pltpu.CompilerParams(
            dimension_semantics=("arbitrary", "arbitrary"),
            vmem_limit_bytes=_vmem_limit(vmem)),
        name="in_proj_lat" if rope else "in_proj_ctx",
    )(*args)


MXU_LOOKAHEAD = 4


def _run_items(items):
    n = len(items)
    state = [None] * n

    def do_issue(i):
        if items[i][0] is not None:
            state[i] = items[i][0]()

    for i in range(min(MXU_LOOKAHEAD, n)):
        do_issue(i)
    for i in range(n):
        if i + MXU_LOOKAHEAD < n:
            do_issue(i + MXU_LOOKAHEAD)
        items[i][1](state[i])
        state[i] = None


def _attend_t_items(streams, finish):
    m, l, o = {}, {}, {}
    items = []
    npieces = len(streams[0][2])
    for pi in range(npieces):
        for tag, q_load, pieces in streams:
            k_load, vt_load = pieces[pi]
            last = pi == npieces - 1

            def issue(k_load=k_load, q_load=q_load):
                return lax.dot_general(k_load(), q_load(), _NT, preferred_element_type=f32)

            def consume(s, tag=tag, vt_load=vt_load, last=last):
                mj = jnp.max(s, axis=0, keepdims=True)
                if tag not in m:
                    m[tag] = mj
                    e = jnp.exp2(s - mj)
                    l[tag] = jnp.sum(e, axis=0, keepdims=True)
                    o[tag] = jnp.dot(vt_load(), e.astype(bf16), preferred_element_type=f32)
                else:
                    mn = jnp.maximum(m[tag], mj)
                    alpha = jnp.exp2(m[tag] - mn)
                    e = jnp.exp2(s - mn)
                    l[tag] = l[tag] * alpha + jnp.sum(e, axis=0, keepdims=True)
                    o[tag] = o[tag] * alpha + jnp.dot(vt_load(), e.astype(bf16),
                                                      preferred_element_type=f32)
                    m[tag] = mn
                if last:
                    finish(tag, o.pop(tag) * (1.0 / l.pop(tag)))

            items.append((issue, consume))
    return items


def _mixer_kernel(*refs, latent, tq, seq, kchunk, lam_init, final):
    it = iter(refs)
    x_ref, qa_ref, ka_ref, va_ref = (next(it) for _ in range(4))
    if latent:
        kca_ref, vca_ref = next(it), next(it)
    qb_ref, kb_ref, vb_ref = (next(it) for _ in range(3))
    if latent:
        kcb_ref, vcb_ref = next(it), next(it)
    uc_ref, sz_ref, dtab_ref, mcat_ref, wout_ref, mod_ref = (next(it) for _ in range(6))
    lamv_ref, dg_ref, sink_ref = (next(it) for _ in range(3))
    if final:
        fg_ref = next(it)
    out_ref = next(it)
    tcat_ref, mix_ref, vt_ref = next(it), next(it), next(it)

    qi = pl.program_id(1)

    @pl.when(qi == 0)
    def _():
        t = jnp.dot(uc_ref[0], mcat_ref[0, 0], preferred_element_type=f32)
        tcat_ref[0:seq, :] = t[:, 0:C_WIDTH].astype(bf16)
        tcat_ref[seq:2 * seq, :] = t[:, C_WIDTH:2 * C_WIDTH].astype(bf16)
        vt_ref[:, 0:seq] = va_ref[0].T
        if latent:
            vt_ref[:, seq:] = vca_ref[0, 0].T

    lv = lamv_ref[...]
    lam = (jnp.exp(jnp.sum(lv[0:1] * lv[1:2], axis=-1, keepdims=True))
           - jnp.exp(jnp.sum(lv[2:3] * lv[3:4], axis=-1, keepdims=True)) + lam_init)

    y_parts = []

    def out_proj_item(c0, c1):
        def consume(_):
            mixed = (mix_ref[:, c0:c1] * sz_ref[0, :, c0:c1].astype(f32)).astype(bf16)
            y_parts.append(jnp.dot(mixed, wout_ref[c0:c1, :], preferred_element_type=f32))
        return (None, consume)

    dg = dg_ref[...]
    key_starts = list(range(0, seq, kchunk))
    first = {}

    def a_finish(tag, out):
        h, c = tag
        if c == 0:
            first[h] = out
            return
        o = (first.pop(h) - lam * out).T
        on = o * lax.rsqrt(jnp.mean(o * o, axis=-1, keepdims=True) + EPS) * dg
        mix_ref[:, h * LANES:(h + 1) * LANES] = on * (1.0 - lam_init)

    def a_items(h):
        hs = slice(h * LANES, (h + 1) * LANES)
        ps = [(lambda j=j: ka_ref[0, j:j + kchunk, hs], lambda j=j: vt_ref[hs, j:j + kchunk])
              for j in key_starts]
        if latent:
            ps.append((lambda: kca_ref[0, 0, :, hs], lambda: vt_ref[hs, seq:]))
        streams = [((h, c), (lambda c=c: qa_ref[0, :, (2 * h + c) * LANES:(2 * h + c + 1) * LANES]),
                    ps) for c in range(2)]
        return _attend_t_items(streams, a_finish)

    lane = lax.broadcasted_iota(jnp.int32, (tq, LANES), 1)
    lo = lane < (LANES // 2)
    if latent:
        win = tq + 2 * WINDOW
        t0 = qi * tq
        ws = pl.multiple_of(jnp.clip(t0 - WINDOW, 0, seq - win), WINDOW)
        qpos = t0 + lax.broadcasted_iota(jnp.int32, (tq, win), 0)
        kpos = ws + lax.broadcasted_iota(jnp.int32, (tq, win), 1)
        band = jnp.abs(kpos - qpos) <= WINDOW
    b_out = {}

    def b_item(kvh, g):
        vs = slice(kvh * LANES, (kvh + 1) * LANES)
        idx = kvh * B_GROUP + g

        def pieces():
            if latent:
                return [(kb_ref[0, pl.ds(ws, win), :], vb_ref[0, pl.ds(ws, win), vs], band),
                        (kcb_ref[0, 0], vcb_ref[0, 0, :, vs], None)]
            return [(kb_ref[0], vb_ref[0, :, vs], None)]

        def issue():
            q = qb_ref[0, :, idx * LANES:(idx + 1) * LANES]
            return [lax.dot_general(q, k, _NT, preferred_element_type=f32) for k, _, _ in pieces()]

        def consume(raw):
            ps = pieces()
            scores = [s if mask is None else jnp.where(mask, s, NEG_INF)
                      for s, (_, _, mask) in zip(raw, ps)]
            sk = jnp.full((1, 1), sink_ref[idx] * LOG2E, f32)
            m = sk
            for s in scores:
                m = jnp.maximum(m, jnp.max(s, axis=-1, keepdims=True))
            l = jnp.exp2(sk - m)
            o = None
            for s, (_, v, _) in zip(scores, ps):
                e = jnp.exp2(s - m)
                l = l + jnp.sum(e, axis=-1, keepdims=True)
                pv = jnp.dot(e.astype(bf16), v, preferred_element_type=f32)
                o = pv if o is None else o + pv
            b_out[(kvh, g)] = o * (1.0 / l)
            if g == B_GROUP - 1:
                mix_ref[:, 512 + kvh * LANES:512 + (kvh + 1) * LANES] = jnp.where(
                    lo, b_out.pop((kvh, 0)), b_out.pop((kvh, 1)))

        return (issue, consume)

    f_acc = []
    n_fs = 4 if seq >= 1024 else 1
    fk = 2 * seq // n_fs

    def f_item(j):
        def issue():
            return jnp.dot(dtab_ref[:, j * fk:(j + 1) * fk], tcat_ref[j * fk:(j + 1) * fk, :],
                           preferred_element_type=f32)

        def consume(p):
            f_acc.append(p if not f_acc else f_acc.pop() + p)
            if j == n_fs - 1:
                mix_ref[:, 768:1024] = f_acc.pop()

        return (issue, consume)

    items = []
    b_list = [b_item(kvh, g) for kvh in range(B_KV_HEADS) for g in range(B_GROUP)]
    f_list = [f_item(j) for j in range(n_fs)]
    for h in range(A_HEADS):
        ah = a_items(h)
        half = len(ah) // 2
        items += ah[:half]
        items.append(b_list[h])
        if h < len(f_list):
            items.append(f_list[h])
        items += ah[half:]
        if h == 1:
            items.append(out_proj_item(0, 256))
    items.append(out_proj_item(512, 1024))
    items.append(out_proj_item(256, 512))
    _run_items(items)

    y = y_parts[0]
    for p in y_parts[1:]:
        y = y + p
    gate = mod_ref[0, :, 2 * D_MODEL:3 * D_MODEL]
    xn = x_ref[0] + gate * y
    if final:
        xn = xn * lax.rsqrt(jnp.mean(xn * xn, axis=-1, keepdims=True) + EPS) * fg_ref[...]
    out_ref[0] = xn


def _mixer(x, proj, cache, mcat_l, path_idx, dtab, wout_bf, mod, lamv, dg, sink_l, lam_init,
           final_g, tq, layer):
    b, s, _ = x.shape
    latent = cache is not None
    final = final_g is not None
    qa, ka, va, qb, kb, vb, uc, sz = proj[:8]
    grid = (b, s // tq)
    qtile = lambda width: pl.BlockSpec((1, tq, width), lambda i, j: (i, j, 0))
    full = lambda width: pl.BlockSpec((1, s, width), lambda i, j: (i, 0, 0))
    const2 = lambda shape: pl.BlockSpec(shape, lambda i, j: (0, 0))
    mod_idx = (lambda i, j: (i, 0, 0)) if mod.shape[0] == b else (lambda i, j: (0, 0, 0))

    in_specs = [qtile(D_MODEL), qtile(1024), full(512), full(512)]
    args = [x, qa, ka, va]
    if latent:
        kca, vca, kcb, vcb = cache
        n_ctx = kca.shape[2]
        cspec = lambda width: pl.BlockSpec((1, 1, n_ctx, width), lambda i, j: (i, layer, 0, 0))
        in_specs += [cspec(512), cspec(512)]
        args += [kca, vca]
    in_specs += [qtile(512), full(128), full(256)]
    args += [qb, kb, vb]
    if latent:
        in_specs += [cspec(128), cspec(256)]
        args += [kcb, vcb]
    in_specs += [
        full(256), qtile(1024),
        pl.BlockSpec((tq, 2 * s), lambda i, j: (j, 0)),
        pl.BlockSpec((1, 1, C_WIDTH, 2 * C_WIDTH), lambda i, j: (layer, path_idx, 0, 0)),
        const2((D_MODEL, D_MODEL)),
        pl.BlockSpec((1, 1, 3 * D_MODEL), mod_idx),
        const2((4, A_QK_DIM)), const2((1, A_V_DIM)),
        pl.BlockSpec(memory_space=pltpu.SMEM),
    ]
    args += [uc, sz, dtab, mcat_l, wout_bf, mod, lamv, dg.reshape(1, A_V_DIM), sink_l]
    if final:
        in_specs.append(const2((1, D_MODEL)))
        args.append(final_g.reshape(1, D_MODEL))

    nk = s + (256 if latent else 0)
    vmem = (2 * (2 * tq * D_MODEL * 4 + tq * 2560 * 2 + s * (512 + 512 + 128 + 256 + 256) * 2
                 + tq * 2 * s * 2 + D_MODEL * D_MODEL * 2)
            + 2 * s * C_WIDTH * 2 + tq * D_MODEL * 4 + 6 * tq * nk * 4 + (8 << 20))
    return pl.pallas_call(
        functools.partial(_mixer_kernel, latent=latent, tq=tq, seq=s, kchunk=min(s, 512),
                          lam_init=lam_init, final=final),
        grid=grid,
        in_specs=in_specs,
        out_specs=qtile(D_MODEL),
        out_shape=jax.ShapeDtypeStruct((b, s, D_MODEL), f32),
        scratch_shapes=[pltpu.VMEM((2 * s, C_WIDTH), bf16), pltpu.VMEM((tq, D_MODEL), f32),
                        pltpu.VMEM((512, nk), bf16)],
        compiler_params=pltpu.CompilerParams(
            dimension_semantics=("arbitrary", "arbitrary"),
            vmem_limit_bytes=_vmem_limit(vmem)),
        name="mixer_lat" if latent else "mixer_ctx",
    )(*args)


def kernel(x_prompt, x_sample, cache_diff_k, cache_diff_v, cache_swa_k, cache_swa_v, c, c_ctx,
           w_ada, b_ada, norm_g, w_in, lam_q1, lam_k1, lam_q2, lam_k2, diff_norm_g, sink,
           w_fourier, w_out, final_g):
    bc, sc_len, _ = x_prompt.shape
    bs, ss_len, _ = x_sample.shape
    n_ctx = cache_diff_k.shape[2]

    rows = 16
    cvec = jnp.zeros((rows, D_MODEL), f32).at[0:bs].set(c).at[bs].set(c_ctx)
    mod = _modulation(cvec, w_ada, b_ada)
    mcat = _fourier_weights(w_fourier, (ss_len, sc_len))

    w_in_bf = w_in.astype(bf16)
    w_out_bf = w_out.astype(bf16)
    rope_tabs = _rope_tables(ss_len)
    dtab_s = _dft_table(ss_len)
    dtab_c = _dft_table(sc_len)

    kca = cache_diff_k.reshape(bs, DEPTH, n_ctx, 512).astype(bf16)
    vca = cache_diff_v.reshape(bs, DEPTH, n_ctx, 512).astype(bf16)
    kcb = cache_swa_k.reshape(bs, DEPTH, n_ctx, 128).astype(bf16)
    vcb = jnp.repeat(cache_swa_v, 2, axis=3).reshape(bs, DEPTH, n_ctx, 256).astype(bf16)

    lamv = jnp.stack([lam_q1, lam_k1, lam_q2, lam_k2], axis=1).astype(f32)
    sink_flat = sink.reshape(DEPTH, B_KV_HEADS * B_GROUP).astype(f32)

    xc, xs = x_prompt, x_sample
    caches = None
    for l in range(DEPTH):
        lam_init = 0.8 - 0.6 * math.exp(-0.3 * l)
        last = l == DEPTH - 1
        mod_c = mod[l, bs:bs + 1][None]
        mod_s = mod[l, 0:bs][:, None, :]

        pc = _in_proj(xc, mod_c, norm_g[l], w_in_bf[l], None, 256, cache_out=(l, caches))
        caches = pc[8:12]
        xc = _mixer(xc, pc, None, mcat, 1, dtab_c, w_out_bf[l], mod_c, lamv[l], diff_norm_g[l],
                    sink_flat[l], lam_init, final_g if last else None, 256, l)

        ps = _in_proj(xs, mod_s, norm_g[l], w_in_bf[l], rope_tabs, 512)
        xs = _mixer(xs, ps, (kca, vca, kcb, vcb), mcat, 0, dtab_s, w_out_bf[l], mod_s, lamv[l],
                    diff_norm_g[l], sink_flat[l], lam_init, final_g if last else None, 256, l)

    new_dk, new_dv, new_sk, new_sv = caches
    swa_shape = (bc, DEPTH, sc_len, B_KV_HEADS, B_HEAD_DIM)
    return (xc, xs, new_dk, new_dv, new_sk.reshape(swa_shape), new_sv.reshape(swa_shape))
```

```python
import functools
import math

import numpy as np
import jax
import jax.numpy as jnp
from jax import lax
from jax.experimental import pallas as pl
from jax.experimental.pallas import tpu as pltpu

D_MODEL = 1024
DEPTH = 4
GRID_W = 64
WINDOW = 128
ROPE_BASE = 10000.0
EPS = 1e-6
NEG_INF = -1e30
A_HEADS = 4
A_V_DIM = 128
A_QK_DIM = 64
B_KV_HEADS = 2
B_GROUP = 2
B_HEAD_DIM = 64
C_GROUPS = 4
C_GROUP_DIM = 64
C_WIDTH = 256

_QA0, _KA0, _VA0, _QB0, _KB0, _VB0, _UC0, _Z0, _IN_COLS = (
    0, 512, 1024, 1536, 1792, 1920, 2048, 2304, 3328)

LANES = 128
V7X_VMEM_BYTES = 64 * 1024 * 1024
LOG2E = 1.4426950408889634
Q_SCALE = (A_QK_DIM ** -0.5) * LOG2E

_NT = (((1,), (1,)), ((), ()))

f32 = jnp.float32
bf16 = jnp.bfloat16


def _vmem_limit(nbytes):
    return int(min(V7X_VMEM_BYTES - (6 << 20), max(32 << 20, nbytes)))


def _mod_kernel(c_ref, w_ref, b_ref, o_ref):
    cv = c_ref[...]
    s = cv * (1.0 / (1.0 + jnp.exp(-cv)))
    o_ref[0] = jnp.dot(s.astype(bf16), w_ref[0].astype(bf16),
                       preferred_element_type=f32) + b_ref[0]


def _modulation(cvec, w_ada, b_ada):
    rows = cvec.shape[0]
    ncol = 3 * D_MODEL
    tn = 1024
    return pl.pallas_call(
        _mod_kernel,
        grid=(DEPTH, ncol // tn),
        in_specs=[
            pl.BlockSpec((rows, D_MODEL), lambda l, j: (0, 0)),
            pl.BlockSpec((1, D_MODEL, tn), lambda l, j: (l, 0, j)),
            pl.BlockSpec((1, 1, tn), lambda l, j: (l, 0, j)),
        ],
        out_specs=pl.BlockSpec((1, rows, tn), lambda l, j: (l, 0, j)),
        out_shape=jax.ShapeDtypeStruct((DEPTH, rows, ncol), f32),
        name="modulation",
    )(cvec, w_ada, b_ada.reshape(DEPTH, 1, ncol))


def _fw_kernel(cc_ref, sc_ref, w_ref, o_ref, *, norms):
    w = w_ref[0]
    m1 = jnp.dot(cc_ref[...], w, preferred_element_type=f32, precision=lax.Precision.HIGHEST)
    m2 = jnp.dot(sc_ref[...], w, preferred_element_type=f32, precision=lax.Precision.HIGHEST)
    for i, nrm in enumerate(norms):
        o_ref[0, i, :, 0:C_WIDTH] = (m1 * nrm).astype(o_ref.dtype)
        o_ref[0, i, :, C_WIDTH:2 * C_WIDTH] = (m2 * nrm).astype(o_ref.dtype)


def _fourier_weights(w_fourier, seq_lens):
    eye = jnp.eye(C_GROUPS, dtype=f32)
    wbd = jnp.einsum('lgcd,gh->lgchd', w_fourier.astype(f32), eye).reshape(DEPTH, C_WIDTH, C_WIDTH)
    ch = np.arange(C_GROUP_DIM)
    ang = 2.0 * np.pi * ((ch[:, None] * ch[None, :]) % C_GROUP_DIM) / C_GROUP_DIM
    bd = np.kron(np.eye(C_GROUPS), np.ones((C_GROUP_DIM, C_GROUP_DIM)))
    cc = jnp.asarray(np.tile(np.cos(ang), (C_GROUPS, C_GROUPS)) * bd, f32)
    sc = jnp.asarray(np.tile(np.sin(ang), (C_GROUPS, C_GROUPS)) * bd, f32)
    norms = tuple(1.0 / math.sqrt(s * C_GROUP_DIM) for s in seq_lens)
    return pl.pallas_call(
        functools.partial(_fw_kernel, norms=norms),
        grid=(DEPTH,),
        in_specs=[
            pl.BlockSpec((C_WIDTH, C_WIDTH), lambda l: (0, 0)),
            pl.BlockSpec((C_WIDTH, C_WIDTH), lambda l: (0, 0)),
            pl.BlockSpec((1, C_WIDTH, C_WIDTH), lambda l: (l, 0, 0)),
        ],
        out_specs=pl.BlockSpec((1, len(norms), C_WIDTH, 2 * C_WIDTH), lambda l: (l, 0, 0, 0)),
        out_shape=jax.ShapeDtypeStruct((DEPTH, len(norms), C_WIDTH, 2 * C_WIDTH), bf16),
        name="fourier_weights",
    )(cc, sc, wbd)


def _dft_table(s):
    n = np.arange(s)
    ang = 2.0 * np.pi * ((n[:, None] * n[None, :]) % s) / s
    tab = np.concatenate([np.cos(ang), -np.sin(ang)], axis=1).astype(np.float32)
    return jnp.asarray(tab).astype(bf16)


def _rope_tables(n_tokens):
    dim = A_QK_DIM
    rows = n_tokens // GRID_W
    row = jnp.repeat(jnp.arange(rows), GRID_W).astype(f32)
    col = jnp.tile(jnp.arange(GRID_W), rows).astype(f32)
    nf = dim // 4
    inv = ROPE_BASE ** (-jnp.arange(nf, dtype=f32) / nf)
    ar = row[:, None] * inv
    ac = col[:, None] * inv
    ang = jnp.concatenate([ar, ar, ac, ac], axis=-1)
    cos = jnp.tile(jnp.cos(ang), (1, LANES // dim))
    sin = jnp.tile(jnp.sin(ang), (1, LANES // dim))
    second = (np.arange(LANES) % (2 * nf)) >= nf
    sin_p = jnp.where(second[None, :], sin, 0.0)
    sin_n = jnp.where(second[None, :], 0.0, -sin)
    return cos.astype(f32), sin_p.astype(f32), sin_n.astype(f32)


def _in_proj_kernel(*refs, rope, emit_cache, n_alias):
    it = iter(refs)
    x_ref, mod_ref, g_ref, w_ref = next(it), next(it), next(it), next(it)
    if rope:
        cos_ref, sp_ref, sn_ref = next(it), next(it), next(it)
    for _ in range(n_alias):
        next(it)
    qa_ref, ka_ref, va_ref, qb_ref, kb_ref, vb_ref, uc_ref, sz_ref = (next(it) for _ in range(8))
    if emit_cache:
        dk_ref, dv_ref, sk_ref, sv_ref = (next(it) for _ in range(4))

    x = x_ref[0]
    ms = jnp.mean(x * x, axis=-1, keepdims=True)
    y = x * lax.rsqrt(ms + EPS) * g_ref[0]
    shift = mod_ref[0, 0, :, 0:D_MODEL]
    scale = mod_ref[0, 0, :, D_MODEL:2 * D_MODEL]
    hb = (y * (1.0 + scale) + shift).astype(bf16)

    def proj(c0, c1):
        return jnp.dot(hb, w_ref[0, :, c0:c1], preferred_element_type=f32)

    lane = lax.broadcasted_iota(jnp.int32, (x.shape[0], LANES), 1)
    lo = lane < (LANES // 2)

    if rope:
        cos, sin_p, sin_n = cos_ref[...], sp_ref[...], sn_ref[...]
        nf = A_QK_DIM // 4

        def rot(t):
            return (t * cos + pltpu.roll(t, nf, 1) * sin_p
                    + pltpu.roll(t, LANES - nf, 1) * sin_n)
    else:
        def rot(t):
            return t

    qa = proj(_QA0, _KA0)
    for h in range(A_HEADS):
        t = rot(qa[:, h * LANES:(h + 1) * LANES]) * Q_SCALE
        qa_ref[0, :, (2 * h) * LANES:(2 * h + 1) * LANES] = jnp.where(lo, t, 0.0).astype(bf16)
        qa_ref[0, :, (2 * h + 1) * LANES:(2 * h + 2) * LANES] = jnp.where(lo, 0.0, t).astype(bf16)

    ka = proj(_KA0, _VA0)
    if emit_cache:
        for h in range(A_HEADS):
            dk_ref[0, 0, :, h, :] = ka[:, h * LANES:(h + 1) * LANES]
    for h in range(A_HEADS):
        ka_ref[0, :, h * LANES:(h + 1) * LANES] = rot(ka[:, h * LANES:(h + 1) * LANES]).astype(bf16)

    va = proj(_VA0, _QB0)
    if emit_cache:
        for h in range(A_HEADS):
            dv_ref[0, 0, :, h, :] = va[:, h * LANES:(h + 1) * LANES]
    va_ref[0] = va.astype(bf16)

    pb = proj(_QB0, _UC0)
    for kvh in range(B_KV_HEADS):
        t = rot(pb[:, kvh * LANES:(kvh + 1) * LANES]) * Q_SCALE
        tr = pltpu.roll(t, LANES // 2, 1)
        if kvh == 0:
            e0, e1 = jnp.where(lo, t, 0.0), jnp.where(lo, tr, 0.0)
        else:
            e0, e1 = jnp.where(lo, 0.0, tr), jnp.where(lo, 0.0, t)
        qb_ref[0, :, (2 * kvh) * LANES:(2 * kvh + 1) * LANES] = e0.astype(bf16)
        qb_ref[0, :, (2 * kvh + 1) * LANES:(2 * kvh + 2) * LANES] = e1.astype(bf16)
    kb = pb[:, 2 * LANES:3 * LANES]
    vb = pb[:, 3 * LANES:4 * LANES]
    if emit_cache:
        sk_ref[0, 0] = kb
        sv_ref[0, 0] = vb
    kb_ref[0] = rot(kb).astype(bf16)
    vbr = pltpu.roll(vb, LANES // 2, 1)
    vb_ref[0, :, 0:LANES] = jnp.where(lo, vb, vbr).astype(bf16)
    vb_ref[0, :, LANES:2 * LANES] = jnp.where(lo, vbr, vb).astype(bf16)

    uc_ref[0] = proj(_UC0, _Z0).astype(bf16)
    z = proj(_Z0, _IN_COLS)
    sz_ref[0] = (z * (1.0 / (1.0 + jnp.exp(-z)))).astype(bf16)


def _mod_spec(layer, mod_row):
    if mod_row is None:
        return pl.BlockSpec((1, 1, 1, 3 * D_MODEL), lambda i, j: (layer, i, 0, 0))
    return pl.BlockSpec((1, 1, 1, 3 * D_MODEL), lambda i, j: (layer, mod_row, 0, 0))


def _in_proj(x, mod, mod_row, norm_g, w_bf, layer, rope_tabs, tm, cache_out=None):
    b, s, _ = x.shape
    rope = rope_tabs is not None
    emit_cache = cache_out is not None
    grid = (b, s // tm)
    tok = lambda width: pl.BlockSpec((1, tm, width), lambda i, j: (i, j, 0))
    in_specs = [
        tok(D_MODEL),
        _mod_spec(layer, mod_row),
        pl.BlockSpec((1, 1, D_MODEL), lambda i, j: (layer, 0, 0)),
        pl.BlockSpec((1, D_MODEL, _IN_COLS), lambda i, j: (layer, 0, 0)),
    ]
    args = [x, mod, norm_g, w_bf]
    if rope:
        in_specs += [pl.BlockSpec((tm, LANES), lambda i, j: (j, 0))] * 3
        args += list(rope_tabs)
    widths = [1024, 512, 512, 512, 128, 256, 256, 1024]
    out_specs = [tok(w) for w in widths]
    out_shape = [jax.ShapeDtypeStruct((b, s, w), bf16) for w in widths]
    aliases = {}
    n_alias = 0
    if emit_cache:
        layer, bufs = cache_out
        out_specs += [
            pl.BlockSpec((1, 1, tm, A_HEADS, LANES), lambda i, j: (i, layer, j, 0, 0)),
            pl.BlockSpec((1, 1, tm, A_HEADS, LANES), lambda i, j: (i, layer, j, 0, 0)),
            pl.BlockSpec((1, 1, tm, LANES), lambda i, j: (i, layer, j, 0)),
            pl.BlockSpec((1, 1, tm, LANES), lambda i, j: (i, layer, j, 0)),
        ]
        out_shape += [jax.ShapeDtypeStruct((b, DEPTH, s, A_HEADS, LANES), f32)] * 2
        out_shape += [jax.ShapeDtypeStruct((b, DEPTH, s, LANES), f32)] * 2
        if bufs is not None:
            n_alias = len(bufs)
            aliases = {len(args) + k: len(widths) + k for k in range(n_alias)}
            in_specs += [pl.BlockSpec(memory_space=pl.ANY)] * n_alias
            args += list(bufs)
    vmem = (2 * tm * D_MODEL * 4 + 2 * D_MODEL * _IN_COLS * 2 + 2 * tm * sum(widths) * 2
            + 6 * tm * LANES * 4 + tm * _IN_COLS * 4 * 2 + (8 << 20))
    return pl.pallas_call(
        functools.partial(_in_proj_kernel, rope=rope, emit_cache=emit_cache, n_alias=n_alias),
        grid=grid,
        in_specs=in_specs,
        out_specs=out_specs,
        out_shape=out_shape,
        input_output_aliases=aliases,
        compiler_params=pltpu.CompilerParams(
            dimension_semantics=("arbitrary", "arbitrary"),
            vmem_limit_bytes=_vmem_limit(vmem)),
        name="in_proj_lat" if rope else "in_proj_ctx",
    )(*args)


MXU_LOOKAHEAD = 4


def _run_items(items):
    n = len(items)
    state = [None] * n

    def do_issue(i):
        if items[i][0] is not None:
            state[i] = items[i][0]()

    for i in range(min(MXU_LOOKAHEAD, n)):
        do_issue(i)
    for i in range(n):
        if i + MXU_LOOKAHEAD < n:
            do_issue(i + MXU_LOOKAHEAD)
        items[i][1](state[i])
        state[i] = None


def _attend_t_items(streams, finish):
    m, l, o = {}, {}, {}
    items = []
    npieces = len(streams[0][2])
    for pi in range(npieces):
        for tag, q_load, pieces in streams:
            k_load, vt_load = pieces[pi]
            last = pi == npieces - 1

            def issue(k_load=k_load, q_load=q_load):
                return lax.dot_general(k_load(), q_load(), _NT, preferred_element_type=f32)

            def consume(s, tag=tag, vt_load=vt_load, last=last):
                mj = jnp.max(s, axis=0, keepdims=True)
                if tag not in m:
                    m[tag] = mj
                    e = jnp.exp2(s - mj)
                    l[tag] = jnp.sum(e, axis=0, keepdims=True)
                    o[tag] = jnp.dot(vt_load(), e.astype(bf16), preferred_element_type=f32)
                else:
                    mn = jnp.maximum(m[tag], mj)
                    alpha = jnp.exp2(m[tag] - mn)
                    e = jnp.exp2(s - mn)
                    l[tag] = l[tag] * alpha + jnp.sum(e, axis=0, keepdims=True)
                    o[tag] = o[tag] * alpha + jnp.dot(vt_load(), e.astype(bf16),
                                                      preferred_element_type=f32)
                    m[tag] = mn
                if last:
                    finish(tag, o.pop(tag) * (1.0 / l.pop(tag)))

            items.append((issue, consume))
    return items


def _mixer_kernel(*refs, latent, tq, seq, kchunk, layer, lam_init, final):
    it = iter(refs)
    x_ref, qa_ref, ka_ref, va_ref = (next(it) for _ in range(4))
    if latent:
        kca_ref, vca_ref = next(it), next(it)
    qb_ref, kb_ref, vb_ref = (next(it) for _ in range(3))
    if latent:
        kcb_ref, vcb_ref = next(it), next(it)
    uc_ref, sz_ref, dtab_ref, mcat_ref, wout_ref, mod_ref = (next(it) for _ in range(6))
    lamv_ref, dg_ref, sink_ref = (next(it) for _ in range(3))
    if final:
        fg_ref = next(it)
    out_ref = next(it)
    tcat_ref, mix_ref, vt_ref = next(it), next(it), next(it)
    if latent:
        kc_s, kcb_s, vcb_s = next(it), next(it), next(it)

    qi = pl.program_id(1)

    @pl.when(qi == 0)
    def _():
        t = jnp.dot(uc_ref[0], mcat_ref[0, 0], preferred_element_type=f32)
        tcat_ref[0:seq, :] = t[:, 0:C_WIDTH].astype(bf16)
        tcat_ref[seq:2 * seq, :] = t[:, C_WIDTH:2 * C_WIDTH].astype(bf16)
        vt_ref[:, 0:seq] = va_ref[0].T
        if latent:
            vt_ref[:, seq:] = vca_ref[0, 0].astype(bf16).T
            kc_s[...] = kca_ref[0, 0].astype(bf16)
            kcb_s[...] = kcb_ref[0, 0].astype(bf16)
            v = vcb_ref[0, 0]
            vr = pltpu.roll(v, LANES // 2, 1)
            first_half = lax.broadcasted_iota(jnp.int32, v.shape, 1) < (LANES // 2)
            vcb_s[:, 0:LANES] = jnp.where(first_half, v, vr).astype(bf16)
            vcb_s[:, LANES:2 * LANES] = jnp.where(first_half, vr, v).astype(bf16)

    lv = lamv_ref[0]
    lam = (jnp.exp(jnp.sum(lv[0:1] * lv[1:2], axis=-1, keepdims=True))
           - jnp.exp(jnp.sum(lv[2:3] * lv[3:4], axis=-1, keepdims=True)) + lam_init)

    y_parts = []

    def out_proj_item(c0, c1):
        def consume(_):
            mixed = (mix_ref[:, c0:c1] * sz_ref[0, :, c0:c1].astype(f32)).astype(bf16)
            y_parts.append(jnp.dot(mixed, wout_ref[0, c0:c1, :], preferred_element_type=f32))
        return (None, consume)

    dg = dg_ref[0]
    key_starts = list(range(0, seq, kchunk))
    first = {}

    def a_finish(tag, out):
        h, c = tag
        if c == 0:
            first[h] = out
            return
        o = (first.pop(h) - lam * out).T
        on = o * lax.rsqrt(jnp.mean(o * o, axis=-1, keepdims=True) + EPS) * dg
        mix_ref[:, h * LANES:(h + 1) * LANES] = on * (1.0 - lam_init)

    def a_items(h):
        hs = slice(h * LANES, (h + 1) * LANES)
        ps = [(lambda j=j: ka_ref[0, j:j + kchunk, hs], lambda j=j: vt_ref[hs, j:j + kchunk])
              for j in key_starts]
        if latent:
            ps.append((lambda: kc_s[:, hs], lambda: vt_ref[hs, seq:]))
        streams = [((h, c), (lambda c=c: qa_ref[0, :, (2 * h + c) * LANES:(2 * h + c + 1) * LANES]),
                    ps) for c in range(2)]
        return _attend_t_items(streams, a_finish)

    lane = lax.broadcasted_iota(jnp.int32, (tq, LANES), 1)
    lo = lane < (LANES // 2)
    if latent:
        win = tq + 2 * WINDOW
        t0 = qi * tq
        ws = pl.multiple_of(jnp.clip(t0 - WINDOW, 0, seq - win), WINDOW)
        qpos = t0 + lax.broadcasted_iota(jnp.int32, (tq, win), 0)
        kpos = ws + lax.broadcasted_iota(jnp.int32, (tq, win), 1)
        band = jnp.abs(kpos - qpos) <= WINDOW
    b_out = {}

    def b_item(kvh, g):
        vs = slice(kvh * LANES, (kvh + 1) * LANES)
        idx = kvh * B_GROUP + g

        def pieces():
            if latent:
                return [(kb_ref[0, pl.ds(ws, win), :], vb_ref[0, pl.ds(ws, win), vs], band),
                        (kcb_s[...], vcb_s[:, vs], None)]
            return [(kb_ref[0], vb_ref[0, :, vs], None)]

        def issue():
            q = qb_ref[0, :, idx * LANES:(idx + 1) * LANES]
            return [lax.dot_general(q, k, _NT, preferred_element_type=f32) for k, _, _ in pieces()]

        def consume(raw):
            ps = pieces()
            scores = [s if mask is None else jnp.where(mask, s, NEG_INF)
                      for s, (_, _, mask) in zip(raw, ps)]
            sk = jnp.full((1, 1), sink_ref[layer, idx] * LOG2E, f32)
            m = sk
            for s in scores:
                m = jnp.maximum(m, jnp.max(s, axis=-1, keepdims=True))
            l = jnp.exp2(sk - m)
            o = None
            for s, (_, v, _) in zip(scores, ps):
                e = jnp.exp2(s - m)
                l = l + jnp.sum(e, axis=-1, keepdims=True)
                pv = jnp.dot(e.astype(bf16), v, preferred_element_type=f32)
                o = pv if o is None else o + pv
            b_out[(kvh, g)] = o * (1.0 / l)
            if g == B_GROUP - 1:
                mix_ref[:, 512 + kvh * LANES:512 + (kvh + 1) * LANES] = jnp.where(
                    lo, b_out.pop((kvh, 0)), b_out.pop((kvh, 1)))

        return (issue, consume)

    f_acc = []
    n_fs = 4 if seq >= 1024 else 1
    fk = 2 * seq // n_fs

    def f_item(j):
        def issue():
            return jnp.dot(dtab_ref[:, j * fk:(j + 1) * fk], tcat_ref[j * fk:(j + 1) * fk, :],
                           preferred_element_type=f32)

        def consume(p):
            f_acc.append(p if not f_acc else f_acc.pop() + p)
            if j == n_fs - 1:
                mix_ref[:, 768:1024] = f_acc.pop()

        return (issue, consume)

    items = []
    b_list = [b_item(kvh, g) for kvh in range(B_KV_HEADS) for g in range(B_GROUP)]
    f_list = [f_item(j) for j in range(n_fs)]
    for h in range(A_HEADS):
        ah = a_items(h)
        half = len(ah) // 2
        items += ah[:half]
        items.append(b_list[h])
        if h < len(f_list):
            items.append(f_list[h])
        items += ah[half:]
        if h == 1:
            items.append(out_proj_item(0, 256))
    items.append(out_proj_item(512, 1024))
    items.append(out_proj_item(256, 512))
    _run_items(items)

    y = y_parts[0]
    for p in y_parts[1:]:
        y = y + p
    gate = mod_ref[0, 0, :, 2 * D_MODEL:3 * D_MODEL]
    xn = x_ref[0] + gate * y
    if final:
        xn = xn * lax.rsqrt(jnp.mean(xn * xn, axis=-1, keepdims=True) + EPS) * fg_ref[...]
    out_ref[0] = xn


def _mixer(x, proj, cache, mcat_l, path_idx, dtab, wout_bf, mod, mod_row, lamv, dg, sink_all,
           lam_init, final_g, tq, layer):
    b, s, _ = x.shape
    latent = cache is not None
    final = final_g is not None
    qa, ka, va, qb, kb, vb, uc, sz = proj[:8]
    grid = (b, s // tq)
    qtile = lambda width: pl.BlockSpec((1, tq, width), lambda i, j: (i, j, 0))
    full = lambda width: pl.BlockSpec((1, s, width), lambda i, j: (i, 0, 0))
    const2 = lambda shape: pl.BlockSpec(shape, lambda i, j: (0, 0))
    per_layer = lambda r, c: pl.BlockSpec((1, r, c), lambda i, j: (layer, 0, 0))

    in_specs = [qtile(D_MODEL), qtile(1024), full(512), full(512)]
    args = [x, qa, ka, va]
    if latent:
        kca, vca, kcb, vcb = cache
        n_ctx = kca.shape[2]
        cspec = lambda width: pl.BlockSpec((1, 1, n_ctx, width), lambda i, j: (i, layer, 0, 0))
        in_specs += [cspec(512), cspec(512)]
        args += [kca, vca]
    in_specs += [qtile(512), full(128), full(256)]
    args += [qb, kb, vb]
    if latent:
        in_specs += [cspec(128), cspec(128)]
        args += [kcb, vcb]
    in_specs += [
        full(256), qtile(1024),
        pl.BlockSpec((tq, 2 * s), lambda i, j: (j, 0)),
        pl.BlockSpec((1, 1, C_WIDTH, 2 * C_WIDTH), lambda i, j: (layer, path_idx, 0, 0)),
        per_layer(D_MODEL, D_MODEL),
        _mod_spec(layer, mod_row),
        per_layer(4, A_QK_DIM), per_layer(1, A_V_DIM),
        pl.BlockSpec(memory_space=pltpu.SMEM),
    ]
    args += [uc, sz, dtab, mcat_l, wout_bf, mod, lamv, dg, sink_all]
    if final:
        in_specs.append(const2((1, D_MODEL)))
        args.append(final_g.reshape(1, D_MODEL))

    nk = s + (256 if latent else 0)
    scratch = [pltpu.VMEM((2 * s, C_WIDTH), bf16), pltpu.VMEM((tq, D_MODEL), f32),
               pltpu.VMEM((512, nk), bf16)]
    if latent:
        scratch += [pltpu.VMEM((n_ctx, 512), bf16), pltpu.VMEM((n_ctx, LANES), bf16),
                    pltpu.VMEM((n_ctx, 2 * LANES), bf16)]
    vmem = (2 * (2 * tq * D_MODEL * 4 + tq * 2560 * 2 + s * (512 + 512 + 128 + 256 + 256) * 2
                 + tq * 2 * s * 2 + D_MODEL * D_MODEL * 2)
            + 2 * s * C_WIDTH * 2 + tq * D_MODEL * 4 + 6 * tq * nk * 4 + (8 << 20))
    return pl.pallas_call(
        functools.partial(_mixer_kernel, latent=latent, tq=tq, seq=s, kchunk=min(s, 512),
                          layer=layer, lam_init=lam_init, final=final),
        grid=grid,
        in_specs=in_specs,
        out_specs=qtile(D_MODEL),
        out_shape=jax.ShapeDtypeStruct((b, s, D_MODEL), f32),
        scratch_shapes=scratch,
        compiler_params=pltpu.CompilerParams(
            dimension_semantics=("arbitrary", "arbitrary"),
            vmem_limit_bytes=_vmem_limit(vmem)),
        name="mixer_lat" if latent else "mixer_ctx",
    )(*args)


def kernel(x_prompt, x_sample, cache_diff_k, cache_diff_v, cache_swa_k, cache_swa_v, c, c_ctx,
           w_ada, b_ada, norm_g, w_in, lam_q1, lam_k1, lam_q2, lam_k2, diff_norm_g, sink,
           w_fourier, w_out, final_g):
    bc, sc_len, _ = x_prompt.shape
    bs, ss_len, _ = x_sample.shape
    n_ctx = cache_diff_k.shape[2]

    rows = 16
    cvec = jnp.zeros((rows, D_MODEL), f32).at[0:bs].set(c).at[bs].set(c_ctx)
    mod = _modulation(cvec, w_ada, b_ada).reshape(DEPTH, rows, 1, 3 * D_MODEL)
    mcat = _fourier_weights(w_fourier, (ss_len, sc_len))

    w_in_bf = w_in.astype(bf16)
    w_out_bf = w_out.astype(bf16)
    rope_tabs = _rope_tables(ss_len)
    dtab_s = _dft_table(ss_len)
    dtab_c = _dft_table(sc_len)

    cache = (cache_diff_k.reshape(bs, DEPTH, n_ctx, 512), cache_diff_v.reshape(bs, DEPTH, n_ctx, 512),
             cache_swa_k.reshape(bs, DEPTH, n_ctx, 128), cache_swa_v.reshape(bs, DEPTH, n_ctx, 128))

    norm_g3 = norm_g.reshape(DEPTH, 1, D_MODEL)
    lamv = jnp.stack([lam_q1, lam_k1, lam_q2, lam_k2], axis=1).astype(f32)
    dg3 = diff_norm_g.reshape(DEPTH, 1, A_V_DIM)
    sink_flat = sink.reshape(DEPTH, B_KV_HEADS * B_GROUP).astype(f32)

    xc, xs = x_prompt, x_sample
    caches = None
    for l in range(DEPTH):
        lam_init = 0.8 - 0.6 * math.exp(-0.3 * l)
        last = l == DEPTH - 1

        pc = _in_proj(xc, mod, bs, norm_g3, w_in_bf, l, None, 256, cache_out=(l, caches))
        caches = pc[8:12]
        xc = _mixer(xc, pc, None, mcat, 1, dtab_c, w_out_bf, mod, bs, lamv, dg3, sink_flat,
                    lam_init, final_g if last else None, 256, l)

        ps = _in_proj(xs, mod, None, norm_g3, w_in_bf, l, rope_tabs, 512)
        xs = _mixer(xs, ps, cache, mcat, 0, dtab_s, w_out_bf, mod, None, lamv, dg3, sink_flat,
                    lam_init, final_g if last else None, 256, l)

    new_dk, new_dv, new_sk, new_sv = caches
    swa_shape = (bc, DEPTH, sc_len, B_KV_HEADS, B_HEAD_DIM)
    return (xc, xs, new_dk, new_dv, new_sk.reshape(swa_shape), new_sv.reshape(swa_shape))
```
